```python
import math
import jax
import jax.numpy as jnp
from jax import lax
import numpy as np

D_MODEL = 2048
BATCH = 4
SEQ = 2048
DEPTH = 4

GRID_W = 64
CTX_LEN = 256
EPS = 1e-6
ROPE_BASE = 10000.0
QBLOCK = 128

A_HEADS = 8
A_HEAD_DIM = 64
A_WIDTH = A_HEADS * 2 * A_HEAD_DIM
POOL_WINDOWS = (2, 4, 8, 16)
POOL_GROUP = 256
B_WIDTH = len(POOL_WINDOWS) * POOL_GROUP
EVEN_IN = 3 * A_WIDTH + B_WIDTH

C_HEADS = 12
Q_LORA = 512
KV_LORA = 512
NOPE_DIM = 128
ROPE_DIM = 64
V_DIM = 128
QK_DIM = NOPE_DIM + ROPE_DIM
C_WIDTH = C_HEADS * V_DIM
F_GROUPS = 4
F_GROUP = 128
D_WIDTH = F_GROUPS * F_GROUP
C_KV_OFF = Q_LORA
C_PE_OFF = Q_LORA + KV_LORA
D_OFF = C_PE_OFF + ROPE_DIM
ODD_IN = D_OFF + D_WIDTH

N_EXPERTS = 32
TOP_K = 4
D_EXPERT = 768
SWIGLU_LIMIT = 7.0
SWIGLU_ALPHA = 1.702
MOE_BLOCK = 128

N_EVEN = (DEPTH + 1) // 2
N_ODD = DEPTH // 2

kernel_name = 'hybrid_diffattn_pool_mla_fourier_moe_dit'


def rmsnorm(x, g):
    xf = x.astype(jnp.float32)
    y = xf * lax.rsqrt(jnp.mean(xf * xf, axis=-1, keepdims=True) + EPS)
    return y.astype(x.dtype) * g


def modulate(h, shift, scale):
    return h * (1 + scale) + shift


def axial_rope(n_tokens, rot_dim, dtype):
    rows = n_tokens // GRID_W
    row = jnp.repeat(jnp.arange(rows, dtype=jnp.float32), GRID_W)
    col = jnp.tile(jnp.arange(GRID_W, dtype=jnp.float32), rows)
    axis_dim = rot_dim // 2
    inv = ROPE_BASE ** (-jnp.arange(0, axis_dim, 2, dtype=jnp.float32) / axis_dim)
    ang = jnp.concatenate([row[:, None] * inv, col[:, None] * inv], axis=-1)
    return jnp.cos(ang).astype(dtype), jnp.sin(ang).astype(dtype)


def apply_rope(x, cos, sin):
    shape = (1, cos.shape[0]) + (1,) * (x.ndim - 3) + (cos.shape[1],)
    cos = cos.reshape(shape)
    sin = sin.reshape(shape)
    x1, x2 = jnp.split(x, 2, axis=-1)
    return jnp.concatenate([x1 * cos - x2 * sin, x2 * cos + x1 * sin], axis=-1)


def sweep_query_blocks(fn, q):
    b, n = q.shape[:2]
    nb = n // QBLOCK
    qb = jnp.moveaxis(q.reshape((b, nb, QBLOCK) + q.shape[2:]), 1, 0)
    out = jnp.moveaxis(lax.map(fn, qb), 0, 1)
    return out.reshape((b, n) + out.shape[3:])


def diff_attend(qb, k, v, lam):
    s = jnp.einsum('bqhmd,bkhmd->bhmqk', qb, k).astype(jnp.float32) * (A_HEAD_DIM ** -0.5)
    p = jax.nn.softmax(s, axis=-1)
    a = p[:, :, 0] - lam * p[:, :, 1]
    return jnp.einsum('bhqk,bkhe->bqhe', a.astype(v.dtype), v)


def softmax_attend(qb, k, v):
    s = jnp.einsum('bqhd,bkhd->bhqk', qb, k).astype(jnp.float32) * (QK_DIM ** -0.5)
    p = jax.nn.softmax(s, axis=-1)
    return jnp.einsum('bhqk,bkhe->bqhe', p.astype(v.dtype), v)


def multiscale_pool(u, w_pool, s_pool):
    b, n, g, cg = u.shape
    uf = u.astype(jnp.float32)
    cs = jnp.concatenate([jnp.zeros((b, 1, g, cg), jnp.float32), jnp.cumsum(uf, axis=1)], axis=1)
    t = jnp.arange(n)
    means = []
    for gi, w in enumerate(POOL_WINDOWS):
        lo = jnp.clip(t - w // 2, 0, n)
        hi = jnp.clip(t + w - w // 2, 0, n)
        win_sum = cs[:, hi, gi] - cs[:, lo, gi]
        means.append(win_sum / (hi - lo).astype(jnp.float32)[None, :, None])
    pooled = (jnp.stack(means, axis=2) - uf).astype(u.dtype)
    mixed = jnp.einsum('bngc,gce->bnge', pooled, w_pool)
    return mixed.reshape(b, n, g * cg) * s_pool


def fourier_mix(u, w_f):
    f = jnp.fft.fft2(u.astype(jnp.float32), axes=(1, 3), norm='ortho').real.astype(u.dtype)
    return jnp.einsum('bngc,gce->bnge', f, w_f).reshape(u.shape[0], u.shape[1], D_WIDTH)


def mixer_ab(p_lat, p_ctx, rope, g_aq, g_ak, lam_p, g_subln, w_pool, s_pool, lam_init, ctx_out):
    cos, sin = rope
    lam_f = lam_p.astype(jnp.float32)
    lam = jnp.exp(jnp.sum(lam_f[0] * lam_f[1])) - jnp.exp(jnp.sum(lam_f[2] * lam_f[3])) + lam_init

    def heads(p, off, g):
        b, n = p.shape[:2]
        return rmsnorm(p[..., off:off + A_WIDTH].reshape(b, n, A_HEADS, 2, A_HEAD_DIM), g)

    def values(p):
        b, n = p.shape[:2]
        return p[..., 2 * A_WIDTH:3 * A_WIDTH].reshape(b, n, A_HEADS, 2 * A_HEAD_DIM)

    def pool_in(p):
        b, n = p.shape[:2]
        return p[..., 3 * A_WIDTH:].reshape(b, n, len(POOL_WINDOWS), POOL_GROUP)

    def finish(o):
        b, n = o.shape[:2]
        return (rmsnorm(o, g_subln) * (1.0 - lam_init)).reshape(b, n, A_WIDTH)

    k_ctx = heads(p_ctx, A_WIDTH, g_ak)
    v_ctx = values(p_ctx)
    q_lat = apply_rope(heads(p_lat, 0, g_aq), cos, sin)
    k_all = jnp.concatenate([apply_rope(heads(p_lat, A_WIDTH, g_ak), cos, sin), k_ctx], axis=1)
    v_all = jnp.concatenate([values(p_lat), v_ctx], axis=1)
    o_lat = sweep_query_blocks(lambda qb: diff_attend(qb, k_all, v_all, lam), q_lat)
    y_lat = jnp.concatenate([finish(o_lat), multiscale_pool(pool_in(p_lat), w_pool, s_pool)], axis=-1)
    if not ctx_out:
        return y_lat, None
    o_ctx = sweep_query_blocks(lambda qb: diff_attend(qb, k_ctx, v_ctx, lam), heads(p_ctx, 0, g_aq))
    y_ctx = jnp.concatenate([finish(o_ctx), multiscale_pool(pool_in(p_ctx), w_pool, s_pool)], axis=-1)
    return y_lat, y_ctx


def rope_tail(x, cos, sin):
    return jnp.concatenate([x[..., :NOPE_DIM], apply_rope(x[..., NOPE_DIM:], cos, sin)], axis=-1)


def mla_q(p, g_qa, w_qb, g_mq, rope):
    b, n = p.shape[:2]
    q = (rmsnorm(p[..., :Q_LORA], g_qa) @ w_qb).reshape(b, n, C_HEADS, QK_DIM)
    q = rmsnorm(q, g_mq)
    return q if rope is None else rope_tail(q, rope[0], rope[1])


def mla_kv(p, g_kva, w_kvb, g_mk, rope):
    b, n = p.shape[:2]
    kv = (rmsnorm(p[..., C_KV_OFF:C_PE_OFF], g_kva) @ w_kvb).reshape(b, n, C_HEADS, NOPE_DIM + V_DIM)
    k_pe = jnp.broadcast_to(p[..., C_PE_OFF:D_OFF][:, :, None, :], (b, n, C_HEADS, ROPE_DIM))
    k = rmsnorm(jnp.concatenate([kv[..., :NOPE_DIM], k_pe], axis=-1), g_mk)
    if rope is not None:
        k = rope_tail(k, rope[0], rope[1])
    return k, kv[..., NOPE_DIM:]


def mixer_cd(p_lat, p_ctx, rope, g_qa, g_kva, w_qb, w_kvb, g_mq, g_mk, w_f, ctx_out):
    b, n = p_lat.shape[:2]
    k_ctx, v_ctx = mla_kv(p_ctx, g_kva, w_kvb, g_mk, None)
    k_lat, v_lat = mla_kv(p_lat, g_kva, w_kvb, g_mk, rope)
    k_all = jnp.concatenate([k_lat, k_ctx], axis=1)
    v_all = jnp.concatenate([v_lat, v_ctx], axis=1)
    q_lat = mla_q(p_lat, g_qa, w_qb, g_mq, rope)
    o_lat = sweep_query_blocks(lambda qb: softmax_attend(qb, k_all, v_all), q_lat).reshape(b, n, C_WIDTH)
    u_lat = p_lat[..., D_OFF:].reshape(b, n, F_GROUPS, F_GROUP)
    y_lat = jnp.concatenate([o_lat, fourier_mix(u_lat, w_f)], axis=-1)
    if not ctx_out:
        return y_lat, None
    nc = p_ctx.shape[1]
    q_ctx = mla_q(p_ctx, g_qa, w_qb, g_mq, None)
    o_ctx = sweep_query_blocks(lambda qb: softmax_attend(qb, k_ctx, v_ctx), q_ctx).reshape(b, nc, C_WIDTH)
    u_ctx = p_ctx[..., D_OFF:].reshape(b, nc, F_GROUPS, F_GROUP)
    y_ctx = jnp.concatenate([o_ctx, fourier_mix(u_ctx, w_f)], axis=-1)
    return y_lat, y_ctx


def moe_ffn(h, w_router, b_router, w_gu, b_gu, w_down, b_down):
    n_tok, d = h.shape
    n_assign = n_tok * TOP_K
    logits = (h @ w_router).astype(jnp.float32) + b_router.astype(jnp.float32)
    top_logit, top_e = lax.top_k(logits, TOP_K)
    gate = jax.nn.softmax(top_logit, axis=-1)
    e_flat = top_e.reshape(-1)
    order = jnp.argsort(e_flat)
    e_sorted = e_flat[order]
    counts = jnp.bincount(e_flat, length=N_EXPERTS)
    padded = (counts + MOE_BLOCK - 1) // MOE_BLOCK * MOE_BLOCK
    pad_end = jnp.cumsum(padded)
    rank = jnp.arange(n_assign) - (jnp.cumsum(counts) - counts)[e_sorted]
    slot = (pad_end - padded)[e_sorted] + rank
    n_blocks = (n_assign + N_EXPERTS * (MOE_BLOCK - 1) + MOE_BLOCK - 1) // MOE_BLOCK
    n_rows = n_blocks * MOE_BLOCK
    tok = jnp.zeros((n_rows,), jnp.int32).at[slot].set((order // TOP_K).astype(jnp.int32))
    wgt = jnp.zeros((n_rows,), jnp.float32).at[slot].set(gate.reshape(-1)[order])
    blk_e = jnp.minimum(jnp.searchsorted(pad_end, jnp.arange(n_blocks) * MOE_BLOCK, side='right'), N_EXPERTS - 1)
    xb = h[tok].reshape(n_blocks, MOE_BLOCK, d)

    def expert_block(args):
        xe, e = args
        gu = xe @ w_gu[e] + b_gu[e]
        g_ = jnp.minimum(gu[..., :D_EXPERT], SWIGLU_LIMIT)
        up = jnp.clip(gu[..., D_EXPERT:], -SWIGLU_LIMIT, SWIGLU_LIMIT)
        act = (up + 1) * (g_ * jax.nn.sigmoid(SWIGLU_ALPHA * g_))
        return act @ w_down[e] + b_down[e]

    yb = lax.map(expert_block, (xb, blk_e)).reshape(n_rows, d)
    return jnp.zeros_like(h).at[tok].add(yb * wgt[:, None].astype(h.dtype))


def setup_inputs(seed: int = 0) -> dict:
    key = jax.random.key(seed)
    ks = iter(jax.random.split(key, 32))
    D = D_MODEL

    def nrm(shape, scale):
        return jax.random.normal(next(ks), shape, jnp.float32) * scale

    def gain(shape):
        return 1.0 + nrm(shape, 0.05)

    return {
        'x': nrm((BATCH, SEQ, D), 1.0),
        'c': nrm((BATCH, D), 1.0),
        'ctx': nrm((BATCH, CTX_LEN, D), 1.0),
        'c_ctx': nrm((D,), 1.0),
        'w_mod': nrm((DEPTH, D, 6 * D), 0.5 * D ** -0.5),
        'b_mod': nrm((DEPTH, 6 * D), 0.02),
        'g_mix': gain((DEPTH, D)),
        'g_ffn': gain((DEPTH, D)),
        'w_out': nrm((DEPTH, D, D), D ** -0.5),
        'w_in_ab': nrm((N_EVEN, D, EVEN_IN), D ** -0.5),
        'g_aq': gain((N_EVEN, A_HEAD_DIM)),
        'g_ak': gain((N_EVEN, A_HEAD_DIM)),
        'lam': nrm((N_EVEN, 4, A_HEAD_DIM), 0.1),
        'g_subln': gain((N_EVEN, 2 * A_HEAD_DIM)),
        'w_pool': nrm((N_EVEN, len(POOL_WINDOWS), POOL_GROUP, POOL_GROUP), POOL_GROUP ** -0.5),
        's_pool': gain((N_EVEN, B_WIDTH)),
        'w_in_cd': nrm((N_ODD, D, ODD_IN), D ** -0.5),
        'g_qa': gain((N_ODD, Q_LORA)),
        'g_kva': gain((N_ODD, KV_LORA)),
        'w_qb': nrm((N_ODD, Q_LORA, C_HEADS * QK_DIM), Q_LORA ** -0.5),
        'w_kvb': nrm((N_ODD, KV_LORA, C_HEADS * (NOPE_DIM + V_DIM)), KV_LORA ** -0.5),
        'g_mq': gain((N_ODD, QK_DIM)),
        'g_mk': gain((N_ODD, QK_DIM)),
        'w_fourier': nrm((N_ODD, F_GROUPS, F_GROUP, F_GROUP), F_GROUP ** -0.5),
        'w_router': nrm((DEPTH, D, N_EXPERTS), D ** -0.5),
        'b_router': nrm((DEPTH, N_EXPERTS), 0.01),
        'w_gu': nrm((DEPTH, N_EXPERTS, D, 2 * D_EXPERT), D ** -0.5),
        'b_gu': nrm((DEPTH, N_EXPERTS, 2 * D_EXPERT), 0.01),
        'w_down': nrm((DEPTH, N_EXPERTS, D_EXPERT, D), D_EXPERT ** -0.5),
        'b_down': nrm((DEPTH, N_EXPERTS, D), 0.01),
    }


def reference(x, c, ctx, c_ctx, w_mod, b_mod, g_mix, g_ffn, w_out, w_in_ab, g_aq, g_ak, lam, g_subln, w_pool, s_pool, w_in_cd, g_qa, g_kva, w_qb, w_kvb, g_mq, g_mk, w_fourier, w_router, b_router, w_gu, b_gu, w_down, b_down):
    b, n, d = x.shape
    n_ctx = ctx.shape[1]
    rope_a = axial_rope(n, A_HEAD_DIM, x.dtype)
    rope_c = axial_rope(n, ROPE_DIM, x.dtype)
    s_lat = jax.nn.silu(c)
    s_ctx = jax.nn.silu(c_ctx)
    xl, xc = x, ctx
    for l in range(DEPTH):
        ctx_out = l < DEPTH - 1
        i = l // 2
        ml = jnp.split((s_lat @ w_mod[l] + b_mod[l])[:, None, :], 6, axis=-1)
        mc = jnp.split(s_ctx @ w_mod[l] + b_mod[l], 6, axis=-1)
        hl = modulate(rmsnorm(xl, g_mix[l]), ml[0], ml[1])
        hc = modulate(rmsnorm(xc, g_mix[l]), mc[0], mc[1])
        if l % 2 == 0:
            lam_init = 0.8 - 0.6 * math.exp(-0.3 * l)
            yl, yc = mixer_ab(hl @ w_in_ab[i], hc @ w_in_ab[i], rope_a, g_aq[i], g_ak[i], lam[i],
                              g_subln[i], w_pool[i], s_pool[i], lam_init, ctx_out)
        else:
            yl, yc = mixer_cd(hl @ w_in_cd[i], hc @ w_in_cd[i], rope_c, g_qa[i], g_kva[i], w_qb[i],
                              w_kvb[i], g_mq[i], g_mk[i], w_fourier[i], ctx_out)
        xl = xl + ml[2] * (yl @ w_out[l])
        fl = modulate(rmsnorm(xl, g_ffn[l]), ml[3], ml[4]).reshape(b * n, d)
        if ctx_out:
            xc = xc + mc[2] * (yc @ w_out[l])
            fc = modulate(rmsnorm(xc, g_ffn[l]), mc[3], mc[4]).reshape(b * n_ctx, d)
            f = moe_ffn(jnp.concatenate([fl, fc], axis=0), w_router[l], b_router[l], w_gu[l], b_gu[l], w_down[l], b_down[l])
            xl = xl + ml[5] * f[:b * n].reshape(b, n, d)
            xc = xc + mc[5] * f[b * n:].reshape(b, n_ctx, d)
        else:
            f = moe_ffn(fl, w_router[l], b_router[l], w_gu[l], b_gu[l], w_down[l], b_down[l])
            xl = xl + ml[5] * f.reshape(b, n, d)
    return xl
```

```python
import functools
import math

import jax
import jax.numpy as jnp
from jax import lax
from jax.experimental import pallas as pl
from jax.experimental.pallas import tpu as pltpu

D_MODEL = 2048
BATCH = 4
SEQ = 2048
DEPTH = 4
GRID_W = 64
CTX_LEN = 256
EPS = 1e-6
ROPE_BASE = 10000.0

A_HEADS = 8
A_HEAD_DIM = 64
A_WIDTH = A_HEADS * 2 * A_HEAD_DIM
POOL_WINDOWS = (2, 4, 8, 16)
POOL_GROUP = 256
B_WIDTH = len(POOL_WINDOWS) * POOL_GROUP
EVEN_IN = 3 * A_WIDTH + B_WIDTH

C_HEADS = 12
Q_LORA = 512
KV_LORA = 512
NOPE_DIM = 128
ROPE_DIM = 64
V_DIM = 128
QK_DIM = NOPE_DIM + ROPE_DIM
C_WIDTH = C_HEADS * V_DIM
F_GROUPS = 4
F_GROUP = 128
D_WIDTH = F_GROUPS * F_GROUP
C_PE_OFF = Q_LORA + KV_LORA
D_OFF = C_PE_OFF + ROPE_DIM
ODD_IN = D_OFF + D_WIDTH

N_EXPERTS = 32
TOP_K = 4
D_EXPERT = 768
SWIGLU_LIMIT = 7.0
SWIGLU_ALPHA = 1.702

T_LAT = BATCH * SEQ
T_CTX = BATCH * CTX_LEN
T_ALL = T_LAT + T_CTX

LANES = 128
Q_PAD = 2 * LANES
ODD_IN_PAD = 13 * LANES
PE_COLBLK = (Q_LORA + KV_LORA + D_WIDTH) // LANES
VMEM_LIMIT = 56 * 1024 * 1024
MOE_TB = 256

BF = jnp.bfloat16
F32 = jnp.float32


def _cparams(*sem):
    return pltpu.CompilerParams(dimension_semantics=sem, vmem_limit_bytes=VMEM_LIMIT)


def _mod_kernel(c_ref, w_ref, b_ref, o_ref):
    c = c_ref[...]
    s = c * jax.nn.sigmoid(c)
    o_ref[0] = jnp.dot(s.astype(BF), w_ref[0].astype(BF), preferred_element_type=F32) + b_ref[0]


def _modulation(cc, w_mod, b_mod):
    d = cc.shape[1]
    n = w_mod.shape[2]
    tn = 1536
    out = pl.pallas_call(
        _mod_kernel,
        out_shape=jax.ShapeDtypeStruct((DEPTH, 8, n), F32),
        grid=(DEPTH, n // tn),
        in_specs=[
            pl.BlockSpec((8, d), lambda l, j: (0, 0)),
            pl.BlockSpec((1, d, tn), lambda l, j: (l, 0, j)),
            pl.BlockSpec((1, 1, tn), lambda l, j: (l, 0, j)),
        ],
        out_specs=pl.BlockSpec((1, 8, tn), lambda l, j: (l, 0, j)),
        compiler_params=_cparams("parallel", "parallel"),
        name="modulation",
    )(cc, w_mod, b_mod.reshape(DEPTH, 1, n))
    return out.reshape(DEPTH, 8, 6, d)


def _row_group(i, tm):
    return jnp.minimum(i * tm // SEQ, BATCH)


def _nm_matmul_kernel(modulate, *refs):
    if modulate:
        x_ref, g_ref, mod_ref, w_ref, o_ref, h_scr = refs
    else:
        x_ref, g_ref, w_ref, o_ref, h_scr = refs

    @pl.when(pl.program_id(1) == 0)
    def _():
        x = x_ref[...]
        y = x * lax.rsqrt(jnp.mean(x * x, axis=-1, keepdims=True) + EPS) * g_ref[...]
        if modulate:
            y = y * (1.0 + mod_ref[0, 1:2, :]) + mod_ref[0, 0:1, :]
        h_scr[...] = y.astype(BF)

    o_ref[...] = jnp.dot(h_scr[...], w_ref[...], preferred_element_type=F32).astype(o_ref.dtype)


def _nm_matmul(x, col_blk, k, g, mod, w, tm, tn, out_dtype, name):
    t = x.shape[0]
    n = w.shape[1]
    modulate = mod is not None
    in_specs = [pl.BlockSpec((tm, k), lambda i, j: (i, col_blk)),
                pl.BlockSpec((1, k), lambda i, j: (0, 0))]
    args = [x, g.reshape(1, k)]
    if modulate:
        in_specs.append(pl.BlockSpec((1, 6, k), lambda i, j: (_row_group(i, tm), 0, 0)))
        args.append(mod)
    in_specs.append(pl.BlockSpec((k, tn), lambda i, j: (0, j)))
    args.append(w)
    return pl.pallas_call(
        functools.partial(_nm_matmul_kernel, modulate),
        out_shape=jax.ShapeDtypeStruct((t, n), out_dtype),
        grid=(t // tm, n // tn),
        in_specs=in_specs,
        out_specs=pl.BlockSpec((tm, tn), lambda i, j: (i, j)),
        scratch_shapes=[pltpu.VMEM((tm, k), BF)],
        compiler_params=_cparams("parallel", "arbitrary"),
        name=name,
    )(*args)


def _rope_tables():
    rows = SEQ // GRID_W
    row = jnp.repeat(jnp.arange(rows, dtype=F32), GRID_W)
    col = jnp.tile(jnp.arange(GRID_W, dtype=F32), rows)
    axis_dim = ROPE_DIM // 2
    inv = ROPE_BASE ** (-jnp.arange(0, axis_dim, 2, dtype=F32) / axis_dim)
    ang = jnp.concatenate([row[:, None] * inv, col[:, None] * inv], axis=-1)
    cos, sin = jnp.cos(ang), jnp.sin(ang)
    cos_a = jnp.concatenate([cos, cos, cos, cos], axis=-1)
    sin_a = jnp.concatenate([-sin, sin, -sin, sin], axis=-1)
    one = jnp.ones((SEQ, LANES - ROPE_DIM), F32)
    cos_c = jnp.concatenate([cos, cos, one], axis=-1)
    sin_c = jnp.concatenate([-sin, sin, 0.0 * one], axis=-1)
    return cos_a, sin_a, cos_c, sin_c


def _swap_halves(x):
    lane = lax.broadcasted_iota(jnp.int32, (1, LANES), 1)
    first = (lane % 64) < 32
    return jnp.where(first, pltpu.roll(x, LANES - 32, 1), pltpu.roll(x, 32, 1))


def _rope(x, c, s):
    return x * c + _swap_halves(x) * s


def _attn_a_kernel(lam_init, nq_lat, with_ctx_q, q_ref, kl_ref, kc_ref, vl_ref, vc_ref, cq_ref, sq_ref,
                   ck_ref, sk_ref, gq_ref, gk_ref, lam_ref, gs_ref, o_ref, kt_scr, v_scr):
    i = pl.program_id(2)
    lane = lax.broadcasted_iota(jnp.int32, (1, LANES), 1)
    map0 = lane < A_HEAD_DIM

    def norm_heads(x, g):
        x2 = x * x
        s0 = jnp.sum(jnp.where(map0, x2, 0.0), axis=-1, keepdims=True)
        s1 = jnp.sum(jnp.where(map0, 0.0, x2), axis=-1, keepdims=True)
        r = jnp.where(map0, lax.rsqrt(s0 / A_HEAD_DIM + EPS), lax.rsqrt(s1 / A_HEAD_DIM + EPS))
        return x * r * g

    @pl.when(i == 0)
    def _():
        gk = gk_ref[...]
        kl = _rope(norm_heads(kl_ref[...], gk), ck_ref[...], sk_ref[...])
        kt_scr[:, 0:SEQ] = kl.T.astype(BF)
        kc = norm_heads(kc_ref[...], gk)
        kt_scr[:, SEQ:SEQ + CTX_LEN] = kc.T.astype(BF)
        v_scr[0:SEQ, :] = vl_ref[...].astype(BF)
        v_scr[SEQ:SEQ + CTX_LEN, :] = vc_ref[...].astype(BF)

    lp = lam_ref[...]
    lam = (jnp.exp(jnp.sum(lp[0:1] * lp[1:2], axis=-1, keepdims=True))
           - jnp.exp(jnp.sum(lp[2:3] * lp[3:4], axis=-1, keepdims=True)) + lam_init)

    def softmax_parts(s):
        m = jnp.max(s, axis=-1, keepdims=True)
        e = jnp.exp(s - m)
        return e, 1.0 / jnp.sum(e, axis=-1, keepdims=True)

    def attend(q, k0, nk):
        q = q * (A_HEAD_DIM ** -0.5)
        q0 = jnp.where(map0, q, 0.0).astype(BF)
        q1 = jnp.where(map0, 0.0, q).astype(BF)
        kt = kt_scr[:, k0:k0 + nk]
        e0, r0 = softmax_parts(jnp.dot(q0, kt, preferred_element_type=F32))
        e1, r1 = softmax_parts(jnp.dot(q1, kt, preferred_element_type=F32))
        a = e0 * r0 - e1 * (lam * r1)
        o = jnp.dot(a.astype(BF), v_scr[k0:k0 + nk, :], preferred_element_type=F32)
        o = o * lax.rsqrt(jnp.mean(o * o, axis=-1, keepdims=True) + EPS) * gs_ref[...]
        o_ref[...] = (o * (1.0 - lam_init)).astype(o_ref.dtype)

    def lat_tile():
        q = _rope(norm_heads(q_ref[...], gq_ref[...]), cq_ref[...], sq_ref[...])
        attend(q, 0, SEQ + CTX_LEN)

    if with_ctx_q:
        pl.when(i < nq_lat)(lat_tile)

        @pl.when(i == nq_lat)
        def _():
            attend(norm_heads(q_ref[...], gq_ref[...]), SEQ, CTX_LEN)
    else:
        lat_tile()


def _attn_a(p, cos_a, sin_a, g_aq, g_ak, lam_p, g_subln, lam_init, with_ctx_q):
    tq = CTX_LEN
    nq_lat = SEQ // tq
    nq = nq_lat + (1 if with_ctx_q else 0)
    ctx_blk = T_LAT // CTX_LEN
    t_out = T_ALL if with_ctx_q else T_LAT
    kcol = A_WIDTH // LANES
    vcol = 2 * A_WIDTH // LANES

    def q_rows(b, h, i):
        return jnp.where(i < nq_lat, b * nq_lat + i, ctx_blk + b)

    tile2 = lambda v: jnp.concatenate([v, v]).reshape(1, LANES)
    return pl.pallas_call(
        functools.partial(_attn_a_kernel, lam_init, nq_lat, with_ctx_q),
        out_shape=jax.ShapeDtypeStruct((t_out, A_WIDTH), BF),
        grid=(BATCH, A_HEADS, nq),
        in_specs=[
            pl.BlockSpec((tq, LANES), lambda b, h, i: (q_rows(b, h, i), h)),
            pl.BlockSpec((SEQ, LANES), lambda b, h, i: (b, kcol + h)),
            pl.BlockSpec((CTX_LEN, LANES), lambda b, h, i: (ctx_blk + b, kcol + h)),
            pl.BlockSpec((SEQ, LANES), lambda b, h, i: (b, vcol + h)),
            pl.BlockSpec((CTX_LEN, LANES), lambda b, h, i: (ctx_blk + b, vcol + h)),
            pl.BlockSpec((tq, LANES), lambda b, h, i: (jnp.minimum(i, nq_lat - 1), 0)),
            pl.BlockSpec((tq, LANES), lambda b, h, i: (jnp.minimum(i, nq_lat - 1), 0)),
            pl.BlockSpec((SEQ, LANES), lambda b, h, i: (0, 0)),
            pl.BlockSpec((SEQ, LANES), lambda b, h, i: (0, 0)),
            pl.BlockSpec((1, LANES), lambda b, h, i: (0, 0)),
            pl.BlockSpec((1, LANES), lambda b, h, i: (0, 0)),
            pl.BlockSpec((4, A_HEAD_DIM), lambda b, h, i: (0, 0)),
            pl.BlockSpec((1, LANES), lambda b, h, i: (0, 0)),
        ],
        out_specs=pl.BlockSpec((tq, LANES), lambda b, h, i: (q_rows(b, h, i), h)),
        scratch_shapes=[pltpu.VMEM((LANES, SEQ + CTX_LEN), BF), pltpu.VMEM((SEQ + CTX_LEN, LANES), BF)],
        compiler_params=_cparams("parallel", "parallel", "arbitrary"),
        name="diff_attention",
    )(p, p, p, p, p, cos_a, sin_a, cos_a, sin_a, tile2(g_aq), tile2(g_ak), lam_p, g_subln.reshape(1, LANES))


POOL_HALO = 8
POOL_CHUNK = 256


def _pool_kernel(seq, u_ref, w_ref, s_ref, o_ref, pad_scr):
    n_chunks = seq // POOL_CHUNK
    win_rows = POOL_CHUNK + 2 * POOL_HALO
    zeros = jnp.zeros((POOL_HALO, POOL_GROUP), F32)
    pad_scr[0:POOL_HALO, :] = zeros
    pad_scr[POOL_HALO + seq:POOL_HALO + seq + POOL_HALO, :] = zeros
    for gi, w in enumerate(POOL_WINDOWS):
        cols = slice(gi * POOL_GROUP, (gi + 1) * POOL_GROUP)
        pad_scr[POOL_HALO:POOL_HALO + seq, :] = u_ref[:, cols]
        w_g = w_ref[gi]
        s_g = s_ref[:, cols]

        def chunk(c, carry, w=w, cols=cols, w_g=w_g, s_g=s_g):
            r0 = pl.multiple_of(c * POOL_CHUNK, POOL_CHUNK)
            win = pad_scr[pl.ds(r0, win_rows), :]
            u = win[POOL_HALO:POOL_HALO + POOL_CHUNK, :]
            acc = u
            for j in range(-(w // 2), w - w // 2):
                if j != 0:
                    acc = acc + pltpu.roll(win, (-j) % win_rows, 0)[POOL_HALO:POOL_HALO + POOL_CHUNK, :]
            t = r0 + lax.broadcasted_iota(jnp.int32, (POOL_CHUNK, 1), 0)
            cnt = jnp.minimum(t + (w - w // 2), seq) - jnp.maximum(t - w // 2, 0)
            pooled = acc / cnt.astype(F32) - u
            y = jnp.dot(pooled.astype(BF), w_g, preferred_element_type=F32) * s_g
            o_ref[pl.ds(r0, POOL_CHUNK), cols] = y.astype(o_ref.dtype)
            return carry

        lax.fori_loop(0, n_chunks, chunk, 0)


def _pool(p, w_pool, s_pool, seq, row_blk0, n_seq):
    ucol = 3 * A_WIDTH // B_WIDTH
    return pl.pallas_call(
        functools.partial(_pool_kernel, seq),
        out_shape=jax.ShapeDtypeStruct((n_seq * seq, B_WIDTH), BF),
        grid=(n_seq,),
        in_specs=[
            pl.BlockSpec((seq, B_WIDTH), lambda b: (row_blk0 + b, ucol)),
            pl.BlockSpec((len(POOL_WINDOWS), POOL_GROUP, POOL_GROUP), lambda b: (0, 0, 0)),
            pl.BlockSpec((1, B_WIDTH), lambda b: (0, 0)),
        ],
        out_specs=pl.BlockSpec((seq, B_WIDTH), lambda b: (b, 0)),
        scratch_shapes=[pltpu.VMEM((seq + 2 * POOL_HALO, POOL_GROUP), F32)],
        compiler_params=_cparams("parallel"),
        name="pool_mixer",
    )(p, w_pool, s_pool.reshape(1, B_WIDTH))


def _attn_c_kernel(nq_lat, with_ctx_q, q_ref, knl_ref, knc_ref, pel_ref, pec_ref, vl_ref, vc_ref, cq_ref,
                   sq_ref, ck_ref, sk_ref, gq_ref, gkn_ref, gkp_ref, o_ref, kt_scr, v_scr):
    i = pl.program_id(2)

    def prep_k(kn, pe, c, s):
        ss = jnp.sum(kn * kn, axis=-1, keepdims=True) + jnp.sum(pe * pe, axis=-1, keepdims=True)
        r = lax.rsqrt(ss / QK_DIM + EPS)
        kn = kn * r * gkn_ref[...]
        pe = pe * r * gkp_ref[...]
        if c is not None:
            pe = _rope(pe, c, s)
        return jnp.concatenate([kn, pe], axis=1)

    @pl.when(i == 0)
    def _():
        kl = prep_k(knl_ref[...], pel_ref[...], ck_ref[...], sk_ref[...])
        kt_scr[:, 0:SEQ] = kl.T.astype(BF)
        kc = prep_k(knc_ref[...], pec_ref[...], None, None)
        kt_scr[:, SEQ:SEQ + CTX_LEN] = kc.T.astype(BF)
        v_scr[0:SEQ, :] = vl_ref[...].astype(BF)
        v_scr[SEQ:SEQ + CTX_LEN, :] = vc_ref[...].astype(BF)

    def prep_q(rope):
        q = q_ref[...]
        r = lax.rsqrt(jnp.sum(q * q, axis=-1, keepdims=True) / QK_DIM + EPS)
        q = q * r * gq_ref[...]
        qn, qp = q[:, :LANES], q[:, LANES:]
        if rope:
            qp = _rope(qp, cq_ref[...], sq_ref[...])
        return (jnp.concatenate([qn, qp], axis=1) * (QK_DIM ** -0.5)).astype(BF)

    def attend(q, k0, nk):
        s = jnp.dot(q, kt_scr[:, k0:k0 + nk], preferred_element_type=F32)
        m = jnp.max(s, axis=-1, keepdims=True)
        e = jnp.exp(s - m)
        l = jnp.sum(e, axis=-1, keepdims=True)
        o = jnp.dot(e.astype(BF), v_scr[k0:k0 + nk, :], preferred_element_type=F32)
        o_ref[...] = (o / l).astype(o_ref.dtype)

    if with_ctx_q:
        @pl.when(i < nq_lat)
        def _():
            attend(prep_q(True), 0, SEQ + CTX_LEN)

        @pl.when(i == nq_lat)
        def _():
            attend(prep_q(False), SEQ, CTX_LEN)
    else:
        attend(prep_q(True), 0, SEQ + CTX_LEN)


def _attn_c(q3, kv3, p, cos_c, sin_c, g_mq, g_mk, with_ctx_q):
    tq = CTX_LEN
    nq_lat = SEQ // tq
    nq = nq_lat + (1 if with_ctx_q else 0)
    ctx_blk = T_LAT // CTX_LEN
    t_out = T_ALL if with_ctx_q else T_LAT

    def q_rows(b, h, i):
        return jnp.where(i < nq_lat, b * nq_lat + i, ctx_blk + b)

    gq = jnp.concatenate([g_mq, jnp.zeros((Q_PAD - QK_DIM,), F32)]).reshape(1, Q_PAD)
    gkn = g_mk[:NOPE_DIM].reshape(1, LANES)
    gkp = jnp.concatenate([g_mk[NOPE_DIM:], jnp.zeros((LANES - ROPE_DIM,), F32)]).reshape(1, LANES)
    return pl.pallas_call(
        functools.partial(_attn_c_kernel, nq_lat, with_ctx_q),
        out_shape=jax.ShapeDtypeStruct((t_out, C_WIDTH), BF),
        grid=(BATCH, C_HEADS, nq),
        in_specs=[
            pl.BlockSpec((tq, Q_PAD), lambda b, h, i: (q_rows(b, h, i), h)),
            pl.BlockSpec((SEQ, LANES), lambda b, h, i: (b, 2 * h)),
            pl.BlockSpec((CTX_LEN, LANES), lambda b, h, i: (ctx_blk + b, 2 * h)),
            pl.BlockSpec((SEQ, LANES), lambda b, h, i: (b, PE_COLBLK)),
            pl.BlockSpec((CTX_LEN, LANES), lambda b, h, i: (ctx_blk + b, PE_COLBLK)),
            pl.BlockSpec((SEQ, LANES), lambda b, h, i: (b, 2 * h + 1)),
            pl.BlockSpec((CTX_LEN, LANES), lambda b, h, i: (ctx_blk + b, 2 * h + 1)),
            pl.BlockSpec((tq, LANES), lambda b, h, i: (jnp.minimum(i, nq_lat - 1), 0)),
            pl.BlockSpec((tq, LANES), lambda b, h, i: (jnp.minimum(i, nq_lat - 1), 0)),
            pl.BlockSpec((SEQ, LANES), lambda b, h, i: (0, 0)),
            pl.BlockSpec((SEQ, LANES), lambda b, h, i: (0, 0)),
            pl.BlockSpec((1, Q_PAD), lambda b, h, i: (0, 0)),
            pl.BlockSpec((1, LANES), lambda b, h, i: (0, 0)),
            pl.BlockSpec((1, LANES), lambda b, h, i: (0, 0)),
        ],
        out_specs=pl.BlockSpec((tq, LANES), lambda b, h, i: (q_rows(b, h, i), h)),
        scratch_shapes=[pltpu.VMEM((Q_PAD, SEQ + CTX_LEN), BF), pltpu.VMEM((SEQ + CTX_LEN, LANES), BF)],
        compiler_params=_cparams("parallel", "parallel", "arbitrary"),
        name="latent_attention",
    )(q3, kv3, kv3, p, p, kv3, kv3, cos_c, sin_c, cos_c, sin_c, gq, gkn, gkp)


def _fourier_in_kernel(u_ref, ab_ref, za_ref, zb_ref):
    for g in range(F_GROUPS):
        cols = slice(g * F_GROUP, (g + 1) * F_GROUP)
        z = jnp.dot(u_ref[:, cols].astype(BF), ab_ref[g], preferred_element_type=F32)
        za_ref[:, cols] = z[:, :F_GROUP].astype(za_ref.dtype)
        zb_ref[:, cols] = z[:, F_GROUP:].astype(zb_ref.dtype)


def _fourier_in(p, ab):
    tm = 512
    ucol = (Q_LORA + KV_LORA) // D_WIDTH
    return pl.pallas_call(
        _fourier_in_kernel,
        out_shape=[jax.ShapeDtypeStruct((T_ALL, D_WIDTH), BF)] * 2,
        grid=(T_ALL // tm,),
        in_specs=[pl.BlockSpec((tm, D_WIDTH), lambda i: (i, ucol)),
                  pl.BlockSpec((F_GROUPS, F_GROUP, 2 * F_GROUP), lambda i: (0, 0, 0))],
        out_specs=[pl.BlockSpec((tm, D_WIDTH), lambda i: (i, 0))] * 2,
        compiler_params=_cparams("parallel"),
        name="fourier_channel_mix",
    )(p, ab)


def _dft_kernel(cn_ref, sn_ref, za_ref, zb_ref, o_ref):
    o = (jnp.dot(cn_ref[...], za_ref[...], preferred_element_type=F32)
         - jnp.dot(sn_ref[...], zb_ref[...], preferred_element_type=F32))
    o_ref[...] = o.astype(o_ref.dtype)


def _dft(cn, sn, za, zb, seq, row_blk0, n_seq):
    tq = min(seq, 512)
    nt = seq // tq
    return pl.pallas_call(
        _dft_kernel,
        out_shape=jax.ShapeDtypeStruct((n_seq * seq, D_WIDTH), BF),
        grid=(nt, n_seq),
        in_specs=[
            pl.BlockSpec((tq, seq), lambda i, b: (i, 0)),
            pl.BlockSpec((tq, seq), lambda i, b: (i, 0)),
            pl.BlockSpec((seq, D_WIDTH), lambda i, b: (row_blk0 + b, 0)),
            pl.BlockSpec((seq, D_WIDTH), lambda i, b: (row_blk0 + b, 0)),
        ],
        out_specs=pl.BlockSpec((tq, D_WIDTH), lambda i, b: (b * nt + i, 0)),
        compiler_params=_cparams("parallel", "arbitrary"),
        name="fourier_position_dft",
    )(cn, sn, za, zb)


def _dft_matrices(n, scale):
    k = jnp.arange(n, dtype=jnp.int32)
    ang = ((k[:, None] * k[None, :]) % n).astype(F32) * (2.0 * math.pi / n)
    return jnp.cos(ang) * scale, jnp.sin(ang) * scale


def _out_ffn_kernel(y1_ref, y2_ref, w1_ref, w2_ref, x_ref, mod_ref, g_ref, wrh_ref, wrl_ref, br_ref,
                    xo_ref, f_ref, te_ref, tg_ref):
    acc = (jnp.dot(y1_ref[...], w1_ref[...], preferred_element_type=F32)
           + jnp.dot(y2_ref[...], w2_ref[...], preferred_element_type=F32))
    x1 = x_ref[...] + mod_ref[0, 2:3, :] * acc
    xo_ref[...] = x1
    y = x1 * lax.rsqrt(jnp.mean(x1 * x1, axis=-1, keepdims=True) + EPS) * g_ref[...]
    f = y * (1.0 + mod_ref[0, 4:5, :]) + mod_ref[0, 3:4, :]
    f_hi = f.astype(BF)
    f_ref[...] = f_hi
    f_lo = (f - f_hi.astype(F32)).astype(BF)
    wrh = wrh_ref[...]
    logits = (jnp.dot(f_hi, wrh, preferred_element_type=F32)
              + jnp.dot(f_lo, wrh, preferred_element_type=F32)
              + jnp.dot(f_hi, wrl_ref[...], preferred_element_type=F32)) + br_ref[...]
    tm = logits.shape[0]
    lane = lax.broadcasted_iota(jnp.int32, (tm, LANES), 1)
    lane_f = lane.astype(F32)
    neg = jnp.float32(-jnp.inf)
    l = jnp.where(lane < N_EXPERTS, logits, neg)
    vals, idxs = [], []
    for _ in range(TOP_K):
        m = jnp.max(l, axis=-1, keepdims=True)
        idx = jnp.min(jnp.where(l == m, lane_f, float(LANES)), axis=-1, keepdims=True)
        vals.append(m)
        idxs.append(idx)
        l = jnp.where(lane_f == idx, neg, l)
    es = [jnp.exp(v - vals[0]) for v in vals]
    tot = es[0] + es[1] + es[2] + es[3]
    te = jnp.zeros((tm, LANES), F32)
    tg = jnp.zeros((tm, LANES), F32)
    for k in range(TOP_K):
        te = jnp.where(lane == k, idxs[k], te)
        tg = jnp.where(lane == k, es[k] / tot, tg)
    te_ref[...] = te.astype(jnp.int32)
    tg_ref[...] = tg


def _out_ffn(y1, y2, w_out, x, mod, g_ffn, w_router, b_router, t_out):
    tm = 256
    k1, k2 = y1.shape[1], y2.shape[1]
    d = D_MODEL
    w1 = w_out[:k1].astype(BF)
    w2 = w_out[k1:].astype(BF)
    wr = jnp.pad(w_router, ((0, 0), (0, LANES - N_EXPERTS)))
    wrh = wr.astype(BF)
    wrl = (wr - wrh.astype(F32)).astype(BF)
    br = jnp.pad(b_router, (0, LANES - N_EXPERTS)).reshape(1, LANES)
    const = lambda i: (0, 0)
    return pl.pallas_call(
        _out_ffn_kernel,
        out_shape=[jax.ShapeDtypeStruct((t_out, d), F32), jax.ShapeDtypeStruct((t_out, d), BF),
                   jax.ShapeDtypeStruct((t_out, LANES), jnp.int32), jax.ShapeDtypeStruct((t_out, LANES), F32)],
        grid=(t_out // tm,),
        in_specs=[
            pl.BlockSpec((tm, k1), lambda i: (i, 0)),
            pl.BlockSpec((tm, k2), lambda i: (i, 0)),
            pl.BlockSpec((k1, d), const),
            pl.BlockSpec((k2, d), const),
            pl.BlockSpec((tm, d), lambda i: (i, 0)),
            pl.BlockSpec((1, 6, d), lambda i: (_row_group(i, tm), 0, 0)),
            pl.BlockSpec((1, d), const),
            pl.BlockSpec((d, LANES), const),
            pl.BlockSpec((d, LANES), const),
            pl.BlockSpec((1, LANES), const),
        ],
        out_specs=[pl.BlockSpec((tm, d), lambda i: (i, 0)), pl.BlockSpec((tm, d), lambda i: (i, 0)),
                   pl.BlockSpec((tm, LANES), lambda i: (i, 0)), pl.BlockSpec((tm, LANES), lambda i: (i, 0))],
        compiler_params=_cparams("parallel"),
        name="out_proj_ffn_prep_router",
    )(y1, y2, w1, w2, x, mod, g_ffn.reshape(1, d), wrh, wrl, br)


def _expert_kernel(be_ref, nu_ref, x_ref, wg_ref, bg_ref, wd_ref, bd_ref, wt_ref, o_ref):
    i = pl.program_id(0)

    @pl.when(i < nu_ref[0])
    def _():
        gu = jnp.dot(x_ref[...], wg_ref[0], preferred_element_type=F32) + bg_ref[0]
        g_ = jnp.minimum(gu[:, :D_EXPERT], SWIGLU_LIMIT)
        up = jnp.clip(gu[:, D_EXPERT:], -SWIGLU_LIMIT, SWIGLU_LIMIT)
        act = (up + 1.0) * (g_ * jax.nn.sigmoid(SWIGLU_ALPHA * g_))
        y = jnp.dot(act.astype(BF), wd_ref[0], preferred_element_type=F32) + bd_ref[0]
        o_ref[...] = (y * wt_ref[...]).astype(o_ref.dtype)

    @pl.when(i >= nu_ref[0])
    def _():
        o_ref[...] = jnp.zeros(o_ref.shape, o_ref.dtype)


def _experts(xg, wgt, blk_e, n_used, w_gu, b_gu, w_down, b_down):
    n_rows, d = xg.shape
    n_blocks = n_rows // MOE_TB
    grid_spec = pltpu.PrefetchScalarGridSpec(
        num_scalar_prefetch=2,
        grid=(n_blocks,),
        in_specs=[
            pl.BlockSpec((MOE_TB, d), lambda i, be, nu: (i, 0)),
            pl.BlockSpec((1, d, 2 * D_EXPERT), lambda i, be, nu: (be[i], 0, 0)),
            pl.BlockSpec((1, 1, 2 * D_EXPERT), lambda i, be, nu: (be[i], 0, 0)),
            pl.BlockSpec((1, D_EXPERT, d), lambda i, be, nu: (be[i], 0, 0)),
            pl.BlockSpec((1, 1, d), lambda i, be, nu: (be[i], 0, 0)),
            pl.BlockSpec((MOE_TB, 1), lambda i, be, nu: (i, 0)),
        ],
        out_specs=pl.BlockSpec((MOE_TB, d), lambda i, be, nu: (i, 0)),
    )
    return pl.pallas_call(
        _expert_kernel,
        out_shape=jax.ShapeDtypeStruct((n_rows, d), F32),
        grid_spec=grid_spec,
        compiler_params=_cparams("arbitrary"),
        name="moe_experts",
    )(blk_e, n_used, xg, w_gu, b_gu.reshape(N_EXPERTS, 1, 2 * D_EXPERT), w_down,
      b_down.reshape(N_EXPERTS, 1, d), wgt.reshape(n_rows, 1))


def _route(top_e, gate):
    t = top_e.shape[0]
    n_assign = t * TOP_K
    n_blocks = (n_assign + N_EXPERTS * (MOE_TB - 1) + MOE_TB - 1) // MOE_TB
    n_rows = n_blocks * MOE_TB
    e_flat = top_e.reshape(-1)
    order = jnp.argsort(e_flat, stable=True).astype(jnp.int32)
    e_sorted = e_flat[order]
    counts = jnp.sum((e_flat[:, None] == jnp.arange(N_EXPERTS, dtype=jnp.int32)[None, :]).astype(jnp.int32), axis=0)
    padded = (counts + MOE_TB - 1) // MOE_TB * MOE_TB
    pad_end = jnp.cumsum(padded)
    rank = jnp.arange(n_assign, dtype=jnp.int32) - (jnp.cumsum(counts) - counts)[e_sorted]
    slot = ((pad_end - padded)[e_sorted] + rank).astype(jnp.int32)
    tok = jnp.zeros((n_rows,), jnp.int32).at[slot].set(order // TOP_K)
    wgt = jnp.zeros((n_rows,), F32).at[slot].set(gate.reshape(-1)[order])
    slot_of = jnp.zeros((n_assign,), jnp.int32).at[order].set(slot).reshape(t, TOP_K)
    blk_e = jnp.minimum(jnp.searchsorted(pad_end, jnp.arange(n_blocks, dtype=jnp.int32) * MOE_TB, side='right'),
                        N_EXPERTS - 1).astype(jnp.int32)
    n_used = (pad_end[-1] // MOE_TB).astype(jnp.int32).reshape(1)
    return tok, wgt, slot_of, blk_e, n_used


def kernel(x, c, ctx, c_ctx, w_mod, b_mod, g_mix, g_ffn, w_out, w_in_ab, g_aq, g_ak, lam, g_subln, w_pool, s_pool, w_in_cd, g_qa, g_kva, w_qb, w_kvb, g_mq, g_mk, w_fourier, w_router, b_router, w_gu, b_gu, w_down, b_down):
    d = D_MODEL
    cc = jnp.concatenate([c, c_ctx[None, :], jnp.zeros((8 - BATCH - 1, d), F32)], axis=0)
    mod_all = _modulation(cc, w_mod, b_mod)
    cos_a, sin_a, cos_c, sin_c = _rope_tables()
    ctx_blk0_pool = T_LAT // CTX_LEN

    xs = jnp.concatenate([x.reshape(T_LAT, d), ctx.reshape(T_CTX, d)], axis=0)
    for l in range(DEPTH):
        ctx_out = l < DEPTH - 1
        t_out = T_ALL if ctx_out else T_LAT
        i = l // 2
        mod = mod_all[l]
        if l % 2 == 0:
            lam_init = 0.8 - 0.6 * math.exp(-0.3 * l)
            p = _nm_matmul(xs, 0, d, g_mix[l], mod, w_in_ab[i].astype(BF), 512, 1024, F32, "in_proj_even")
            y1 = _attn_a(p, cos_a, sin_a, g_aq[i], g_ak[i], lam[i], g_subln[i], lam_init, ctx_out)
            wp = w_pool[i].astype(BF)
            y2 = _pool(p, wp, s_pool[i], SEQ, 0, BATCH)
            if ctx_out:
                y2 = jnp.concatenate([y2, _pool(p, wp, s_pool[i], CTX_LEN, ctx_blk0_pool, BATCH)], axis=0)
        else:
            w_in = w_in_cd[i]
            w_in = jnp.concatenate([w_in[:, :C_PE_OFF], w_in[:, D_OFF:], w_in[:, C_PE_OFF:D_OFF],
                                    jnp.zeros((d, ODD_IN_PAD - ODD_IN), F32)], axis=1).astype(BF)
            p = _nm_matmul(xs, 0, d, g_mix[l], mod, w_in, 512, ODD_IN_PAD, F32, "in_proj_odd")
            wq = jnp.pad(w_qb[i].reshape(Q_LORA, C_HEADS, QK_DIM), ((0, 0), (0, 0), (0, Q_PAD - QK_DIM)))
            wq = wq.reshape(Q_LORA, C_HEADS * Q_PAD).astype(BF)
            q3 = _nm_matmul(p, 0, Q_LORA, g_qa[i], None, wq, 512, C_HEADS * Q_PAD, F32, "mla_q_up")
            kv3 = _nm_matmul(p, 1, KV_LORA, g_kva[i], None, w_kvb[i].astype(BF), 512,
                             C_HEADS * (NOPE_DIM + V_DIM), F32, "mla_kv_up")
            y1 = _attn_c(q3, kv3, p, cos_c, sin_c, g_mq[i], g_mk[i], ctx_out)
            cc_m, sc_m = _dft_matrices(F_GROUP, 1.0)
            hp = lax.Precision.HIGHEST
            ab = jnp.concatenate([jnp.einsum('ck,gke->gce', cc_m, w_fourier[i], precision=hp),
                                  jnp.einsum('ck,gke->gce', sc_m, w_fourier[i], precision=hp)], axis=-1).astype(BF)
            za, zb = _fourier_in(p, ab)
            cn, sn = _dft_matrices(SEQ, (SEQ * F_GROUP) ** -0.5)
            y2 = _dft(cn.astype(BF), sn.astype(BF), za, zb, SEQ, 0, BATCH)
            if ctx_out:
                cn_c, sn_c = _dft_matrices(CTX_LEN, (CTX_LEN * F_GROUP) ** -0.5)
                y2 = jnp.concatenate([y2, _dft(cn_c.astype(BF), sn_c.astype(BF), za, zb, CTX_LEN,
                                               T_LAT // CTX_LEN, BATCH)], axis=0)

        x1, f, te, tg = _out_ffn(y1, y2, w_out[l], xs, mod, g_ffn[l], w_router[l], b_router[l], t_out)
        tok, wgt, slot_of, blk_e, n_used = _route(te[:, :TOP_K], tg[:, :TOP_K])
        xg = f[tok]
        yb = _experts(xg, wgt, blk_e, n_used, w_gu[l].astype(BF), b_gu[l], w_down[l].astype(BF), b_down[l])
        moe = jnp.sum(yb[slot_of.reshape(-1)].reshape(t_out, TOP_K, d), axis=1)
        gate_rows = jnp.repeat(mod[:BATCH, 5, :], SEQ, axis=0)
        if ctx_out:
            gate_rows = jnp.concatenate([gate_rows, jnp.broadcast_to(mod[BATCH, 5, :], (T_CTX, d))], axis=0)
        xs = x1 + gate_rows * moe
    return xs.reshape(BATCH, SEQ, d)
```

```python
import functools
import math

import jax
import jax.numpy as jnp
from jax import lax
from jax.experimental import pallas as pl
from jax.experimental.pallas import tpu as pltpu

D_MODEL = 2048
BATCH = 4
SEQ = 2048
DEPTH = 4
GRID_W = 64
CTX_LEN = 256
EPS = 1e-6
ROPE_BASE = 10000.0

A_HEADS = 8
A_HEAD_DIM = 64
A_WIDTH = A_HEADS * 2 * A_HEAD_DIM
POOL_WINDOWS = (2, 4, 8, 16)
POOL_GROUP = 256
B_WIDTH = len(POOL_WINDOWS) * POOL_GROUP
EVEN_IN = 3 * A_WIDTH + B_WIDTH

C_HEADS = 12
Q_LORA = 512
KV_LORA = 512
NOPE_DIM = 128
ROPE_DIM = 64
V_DIM = 128
QK_DIM = NOPE_DIM + ROPE_DIM
C_WIDTH = C_HEADS * V_DIM
F_GROUPS = 4
F_GROUP = 128
D_WIDTH = F_GROUPS * F_GROUP
C_PE_OFF = Q_LORA + KV_LORA
D_OFF = C_PE_OFF + ROPE_DIM
ODD_IN = D_OFF + D_WIDTH

N_EXPERTS = 32
TOP_K = 4
D_EXPERT = 768
SWIGLU_LIMIT = 7.0
SWIGLU_ALPHA = 1.702

T_LAT = BATCH * SEQ
T_CTX = BATCH * CTX_LEN
T_ALL = T_LAT + T_CTX

LANES = 128
Q_PAD = 2 * LANES
ODD_IN_PAD = 13 * LANES
PE_COLBLK = (Q_LORA + KV_LORA + D_WIDTH) // LANES
VMEM_LIMIT = 56 * 1024 * 1024
EXPERT_VMEM_LIMIT = 60 * 1024 * 1024
MOE_TB = 128

BF = jnp.bfloat16
F32 = jnp.float32


def _cparams(*sem):
    return pltpu.CompilerParams(dimension_semantics=sem, vmem_limit_bytes=VMEM_LIMIT)


SLAB = D_MODEL // LANES


def _store_slabs(ref, val):
    n = val.shape[0]
    for s in range(SLAB):
        ref[pl.ds(s, n, stride=SLAB), :] = val[:, s * LANES:(s + 1) * LANES]


def _load_slabs(ref, n):
    return jnp.concatenate([ref[pl.ds(s, n, stride=SLAB), :] for s in range(SLAB)], axis=1)


def _mod_kernel(c_ref, w_ref, b_ref, o_ref):
    c = c_ref[...]
    s = c * jax.nn.sigmoid(c)
    o_ref[0] = jnp.dot(s.astype(BF), w_ref[0].astype(BF), preferred_element_type=F32) + b_ref[0]


def _modulation(cc, w_mod, b_mod):
    d = cc.shape[1]
    n = w_mod.shape[2]
    tn = 1536
    out = pl.pallas_call(
        _mod_kernel,
        out_shape=jax.ShapeDtypeStruct((DEPTH, 8, n), F32),
        grid=(DEPTH, n // tn),
        in_specs=[
            pl.BlockSpec((8, d), lambda l, j: (0, 0)),
            pl.BlockSpec((1, d, tn), lambda l, j: (l, 0, j)),
            pl.BlockSpec((1, 1, tn), lambda l, j: (l, 0, j)),
        ],
        out_specs=pl.BlockSpec((1, 8, tn), lambda l, j: (l, 0, j)),
        compiler_params=_cparams("parallel", "parallel"),
        name="modulation",
    )(cc, w_mod, b_mod.reshape(DEPTH, 1, n))
    return out.reshape(DEPTH, 8, 6, d)


def _row_group(i, tm):
    return jnp.minimum(i * tm // SEQ, BATCH)


def _nm_matmul_kernel(modulate, *refs):
    if modulate:
        x_ref, g_ref, mod_ref, w_ref, o_ref, h_scr = refs
    else:
        x_ref, g_ref, w_ref, o_ref, h_scr = refs

    @pl.when(pl.program_id(1) == 0)
    def _():
        x = x_ref[...]
        y = x * lax.rsqrt(jnp.mean(x * x, axis=-1, keepdims=True) + EPS) * g_ref[...]
        if modulate:
            y = y * (1.0 + mod_ref[0, 1:2, :]) + mod_ref[0, 0:1, :]
        h_scr[...] = y.astype(BF)

    o_ref[...] = jnp.dot(h_scr[...], w_ref[0], preferred_element_type=F32).astype(o_ref.dtype)


def _nm_matmul(x, col_blk, k, g, mod, w, wl, tm, tn, out_dtype, name):
    t = x.shape[0]
    n = w.shape[2]
    modulate = mod is not None
    in_specs = [pl.BlockSpec((tm, k), lambda i, j: (i, col_blk)),
                pl.BlockSpec((1, k), lambda i, j: (0, 0))]
    args = [x, g.reshape(1, k)]
    if modulate:
        in_specs.append(pl.BlockSpec((1, 6, k), lambda i, j: (_row_group(i, tm), 0, 0)))
        args.append(mod)
    in_specs.append(pl.BlockSpec((1, k, tn), lambda i, j: (wl, 0, j)))
    args.append(w)
    return pl.pallas_call(
        functools.partial(_nm_matmul_kernel, modulate),
        out_shape=jax.ShapeDtypeStruct((t, n), out_dtype),
        grid=(t // tm, n // tn),
        in_specs=in_specs,
        out_specs=pl.BlockSpec((tm, tn), lambda i, j: (i, j)),
        scratch_shapes=[pltpu.VMEM((tm, k), BF)],
        compiler_params=_cparams("parallel", "arbitrary"),
        name=name,
    )(*args)


def _rope_tables():
    rows = SEQ // GRID_W
    row = jnp.repeat(jnp.arange(rows, dtype=F32), GRID_W)
    col = jnp.tile(jnp.arange(GRID_W, dtype=F32), rows)
    axis_dim = ROPE_DIM // 2
    inv = ROPE_BASE ** (-jnp.arange(0, axis_dim, 2, dtype=F32) / axis_dim)
    ang = jnp.concatenate([row[:, None] * inv, col[:, None] * inv], axis=-1)
    cos, sin = jnp.cos(ang), jnp.sin(ang)
    cos_a = jnp.concatenate([cos, cos, cos, cos], axis=-1)
    sin_a = jnp.concatenate([-sin, sin, -sin, sin], axis=-1)
    one = jnp.ones((SEQ, LANES - ROPE_DIM), F32)
    cos_c = jnp.concatenate([cos, cos, one], axis=-1)
    sin_c = jnp.concatenate([-sin, sin, 0.0 * one], axis=-1)
    return cos_a, sin_a, cos_c, sin_c


def _swap_halves(x):
    lane = lax.broadcasted_iota(jnp.int32, (1, LANES), 1)
    first = (lane % 64) < 32
    return jnp.where(first, pltpu.roll(x, LANES - 32, 1), pltpu.roll(x, 32, 1))


def _rope(x, c, s):
    return x * c + _swap_halves(x) * s


def _attn_a_kernel(lam_init, nq_lat, with_ctx_q, q_ref, kl_ref, kc_ref, vl_ref, vc_ref, cq_ref, sq_ref,
                   ck_ref, sk_ref, gq_ref, gk_ref, lam_ref, gs_ref, o_ref, kt_scr, v_scr):
    i = pl.program_id(2)
    lane = lax.broadcasted_iota(jnp.int32, (1, LANES), 1)
    map0 = lane < A_HEAD_DIM

    def norm_heads(x, g):
        x2 = x * x
        s0 = jnp.sum(jnp.where(map0, x2, 0.0), axis=-1, keepdims=True)
        s1 = jnp.sum(jnp.where(map0, 0.0, x2), axis=-1, keepdims=True)
        r = jnp.where(map0, lax.rsqrt(s0 / A_HEAD_DIM + EPS), lax.rsqrt(s1 / A_HEAD_DIM + EPS))
        return x * r * g

    @pl.when(i == 0)
    def _():
        gk = gk_ref[...]
        kl = _rope(norm_heads(kl_ref[...], gk), ck_ref[...], sk_ref[...])
        kt_scr[:, 0:SEQ] = kl.T.astype(BF)
        kc = norm_heads(kc_ref[...], gk)
        kt_scr[:, SEQ:SEQ + CTX_LEN] = kc.T.astype(BF)
        v_scr[0:SEQ, :] = vl_ref[...].astype(BF)
        v_scr[SEQ:SEQ + CTX_LEN, :] = vc_ref[...].astype(BF)

    lp = lam_ref[...]
    lam = (jnp.exp(jnp.sum(lp[0:1] * lp[1:2], axis=-1, keepdims=True))
           - jnp.exp(jnp.sum(lp[2:3] * lp[3:4], axis=-1, keepdims=True)) + lam_init)

    def softmax_parts(s):
        m = jnp.max(s, axis=-1, keepdims=True)
        e = jnp.exp(s - m)
        return e, 1.0 / jnp.sum(e, axis=-1, keepdims=True)

    def attend(q, k0, nk):
        q = q * (A_HEAD_DIM ** -0.5)
        q0 = jnp.where(map0, q, 0.0).astype(BF)
        q1 = jnp.where(map0, 0.0, q).astype(BF)
        kt = kt_scr[:, k0:k0 + nk]
        e0, r0 = softmax_parts(jnp.dot(q0, kt, preferred_element_type=F32))
        e1, r1 = softmax_parts(jnp.dot(q1, kt, preferred_element_type=F32))
        a = e0 * r0 - e1 * (lam * r1)
        o = jnp.dot(a.astype(BF), v_scr[k0:k0 + nk, :], preferred_element_type=F32)
        o = o * lax.rsqrt(jnp.mean(o * o, axis=-1, keepdims=True) + EPS) * gs_ref[...]
        o_ref[...] = (o * (1.0 - lam_init)).astype(o_ref.dtype)

    def lat_tile():
        q = _rope(norm_heads(q_ref[...], gq_ref[...]), cq_ref[...], sq_ref[...])
        attend(q, 0, SEQ + CTX_LEN)

    if with_ctx_q:
        pl.when(i < nq_lat)(lat_tile)

        @pl.when(i == nq_lat)
        def _():
            attend(norm_heads(q_ref[...], gq_ref[...]), SEQ, CTX_LEN)
    else:
        lat_tile()


def _attn_a(p, cos_a, sin_a, g_aq, g_ak, lam_p, g_subln, lam_init, with_ctx_q):
    tq = CTX_LEN
    nq_lat = SEQ // tq
    nq = nq_lat + (1 if with_ctx_q else 0)
    ctx_blk = T_LAT // CTX_LEN
    t_out = T_ALL if with_ctx_q else T_LAT
    kcol = A_WIDTH // LANES
    vcol = 2 * A_WIDTH // LANES

    def q_rows(b, h, i):
        return jnp.where(i < nq_lat, b * nq_lat + i, ctx_blk + b)

    tile2 = lambda v: jnp.concatenate([v, v]).reshape(1, LANES)
    return pl.pallas_call(
        functools.partial(_attn_a_kernel, lam_init, nq_lat, with_ctx_q),
        out_shape=jax.ShapeDtypeStruct((t_out, A_WIDTH), BF),
        grid=(BATCH, A_HEADS, nq),
        in_specs=[
            pl.BlockSpec((tq, LANES), lambda b, h, i: (q_rows(b, h, i), h)),
            pl.BlockSpec((SEQ, LANES), lambda b, h, i: (b, kcol + h)),
            pl.BlockSpec((CTX_LEN, LANES), lambda b, h, i: (ctx_blk + b, kcol + h)),
            pl.BlockSpec((SEQ, LANES), lambda b, h, i: (b, vcol + h)),
            pl.BlockSpec((CTX_LEN, LANES), lambda b, h, i: (ctx_blk + b, vcol + h)),
            pl.BlockSpec((tq, LANES), lambda b, h, i: (jnp.minimum(i, nq_lat - 1), 0)),
            pl.BlockSpec((tq, LANES), lambda b, h, i: (jnp.minimum(i, nq_lat - 1), 0)),
            pl.BlockSpec((SEQ, LANES), lambda b, h, i: (0, 0)),
            pl.BlockSpec((SEQ, LANES), lambda b, h, i: (0, 0)),
            pl.BlockSpec((1, LANES), lambda b, h, i: (0, 0)),
            pl.BlockSpec((1, LANES), lambda b, h, i: (0, 0)),
            pl.BlockSpec((4, A_HEAD_DIM), lambda b, h, i: (0, 0)),
            pl.BlockSpec((1, LANES), lambda b, h, i: (0, 0)),
        ],
        out_specs=pl.BlockSpec((tq, LANES), lambda b, h, i: (q_rows(b, h, i), h)),
        scratch_shapes=[pltpu.VMEM((LANES, SEQ + CTX_LEN), BF), pltpu.VMEM((SEQ + CTX_LEN, LANES), BF)],
        compiler_params=_cparams("parallel", "parallel", "arbitrary"),
        name="diff_attention",
    )(p, p, p, p, p, cos_a, sin_a, cos_a, sin_a, tile2(g_aq), tile2(g_ak), lam_p, g_subln.reshape(1, LANES))


POOL_HALO = 8
POOL_CHUNK = 256


def _pool_kernel(seq, u_ref, w_ref, s_ref, o_ref, pad_scr):
    n_chunks = seq // POOL_CHUNK
    win_rows = POOL_CHUNK + 2 * POOL_HALO
    zeros = jnp.zeros((POOL_HALO, POOL_GROUP), F32)
    pad_scr[0:POOL_HALO, :] = zeros
    pad_scr[POOL_HALO + seq:POOL_HALO + seq + POOL_HALO, :] = zeros
    for gi, w in enumerate(POOL_WINDOWS):
        cols = slice(gi * POOL_GROUP, (gi + 1) * POOL_GROUP)
        pad_scr[POOL_HALO:POOL_HALO + seq, :] = u_ref[:, cols]
        w_g = w_ref[gi]
        s_g = s_ref[:, cols]

        def chunk(c, carry, w=w, cols=cols, w_g=w_g, s_g=s_g):
            r0 = pl.multiple_of(c * POOL_CHUNK, POOL_CHUNK)
            win = pad_scr[pl.ds(r0, win_rows), :]
            u = win[POOL_HALO:POOL_HALO + POOL_CHUNK, :]
            acc = u
            for j in range(-(w // 2), w - w // 2):
                if j != 0:
                    acc = acc + pltpu.roll(win, (-j) % win_rows, 0)[POOL_HALO:POOL_HALO + POOL_CHUNK, :]
            t = r0 + lax.broadcasted_iota(jnp.int32, (POOL_CHUNK, 1), 0)
            cnt = jnp.minimum(t + (w - w // 2), seq) - jnp.maximum(t - w // 2, 0)
            pooled = acc / cnt.astype(F32) - u
            y = jnp.dot(pooled.astype(BF), w_g, preferred_element_type=F32) * s_g
            o_ref[pl.ds(r0, POOL_CHUNK), cols] = y.astype(o_ref.dtype)
            return carry

        lax.fori_loop(0, n_chunks, chunk, 0)


def _pool(p, w_pool, s_pool, seq, row_blk0, n_seq):
    ucol = 3 * A_WIDTH // B_WIDTH
    return pl.pallas_call(
        functools.partial(_pool_kernel, seq),
        out_shape=jax.ShapeDtypeStruct((n_seq * seq, B_WIDTH), BF),
        grid=(n_seq,),
        in_specs=[
            pl.BlockSpec((seq, B_WIDTH), lambda b: (row_blk0 + b, ucol)),
            pl.BlockSpec((len(POOL_WINDOWS), POOL_GROUP, POOL_GROUP), lambda b: (0, 0, 0)),
            pl.BlockSpec((1, B_WIDTH), lambda b: (0, 0)),
        ],
        out_specs=pl.BlockSpec((seq, B_WIDTH), lambda b: (b, 0)),
        scratch_shapes=[pltpu.VMEM((seq + 2 * POOL_HALO, POOL_GROUP), F32)],
        compiler_params=_cparams("parallel"),
        name="pool_mixer",
    )(p, w_pool, s_pool.reshape(1, B_WIDTH))


def _attn_c_kernel(nq_lat, with_ctx_q, q_ref, knl_ref, knc_ref, pel_ref, pec_ref, vl_ref, vc_ref, cq_ref,
                   sq_ref, ck_ref, sk_ref, gq_ref, gkn_ref, gkp_ref, o_ref, kt_scr, v_scr):
    i = pl.program_id(2)

    def prep_k(kn, pe, c, s):
        ss = jnp.sum(kn * kn, axis=-1, keepdims=True) + jnp.sum(pe * pe, axis=-1, keepdims=True)
        r = lax.rsqrt(ss / QK_DIM + EPS)
        kn = kn * r * gkn_ref[...]
        pe = pe * r * gkp_ref[...]
        if c is not None:
            pe = _rope(pe, c, s)
        return jnp.concatenate([kn, pe], axis=1)

    @pl.when(i == 0)
    def _():
        kl = prep_k(knl_ref[...], pel_ref[...], ck_ref[...], sk_ref[...])
        kt_scr[:, 0:SEQ] = kl.T.astype(BF)
        kc = prep_k(knc_ref[...], pec_ref[...], None, None)
        kt_scr[:, SEQ:SEQ + CTX_LEN] = kc.T.astype(BF)
        v_scr[0:SEQ, :] = vl_ref[...].astype(BF)
        v_scr[SEQ:SEQ + CTX_LEN, :] = vc_ref[...].astype(BF)

    def prep_q(rope):
        q = q_ref[...]
        r = lax.rsqrt(jnp.sum(q * q, axis=-1, keepdims=True) / QK_DIM + EPS)
        q = q * r * gq_ref[...]
        qn, qp = q[:, :LANES], q[:, LANES:]
        if rope:
            qp = _rope(qp, cq_ref[...], sq_ref[...])
        return (jnp.concatenate([qn, qp], axis=1) * (QK_DIM ** -0.5)).astype(BF)

    def attend(q, k0, nk):
        s = jnp.dot(q, kt_scr[:, k0:k0 + nk], preferred_element_type=F32)
        m = jnp.max(s, axis=-1, keepdims=True)
        e = jnp.exp(s - m)
        l = jnp.sum(e, axis=-1, keepdims=True)
        o = jnp.dot(e.astype(BF), v_scr[k0:k0 + nk, :], preferred_element_type=F32)
        o_ref[...] = (o / l).astype(o_ref.dtype)

    if with_ctx_q:
        @pl.when(i < nq_lat)
        def _():
            attend(prep_q(True), 0, SEQ + CTX_LEN)

        @pl.when(i == nq_lat)
        def _():
            attend(prep_q(False), SEQ, CTX_LEN)
    else:
        attend(prep_q(True), 0, SEQ + CTX_LEN)


def _attn_c(q3, kv3, p, cos_c, sin_c, g_mq, g_mk, with_ctx_q):
    tq = CTX_LEN
    nq_lat = SEQ // tq
    nq = nq_lat + (1 if with_ctx_q else 0)
    ctx_blk = T_LAT // CTX_LEN
    t_out = T_ALL if with_ctx_q else T_LAT

    def q_rows(b, h, i):
        return jnp.where(i < nq_lat, b * nq_lat + i, ctx_blk + b)

    gq = jnp.concatenate([g_mq, jnp.zeros((Q_PAD - QK_DIM,), F32)]).reshape(1, Q_PAD)
    gkn = g_mk[:NOPE_DIM].reshape(1, LANES)
    gkp = jnp.concatenate([g_mk[NOPE_DIM:], jnp.zeros((LANES - ROPE_DIM,), F32)]).reshape(1, LANES)
    return pl.pallas_call(
        functools.partial(_attn_c_kernel, nq_lat, with_ctx_q),
        out_shape=jax.ShapeDtypeStruct((t_out, C_WIDTH), BF),
        grid=(BATCH, C_HEADS, nq),
        in_specs=[
            pl.BlockSpec((tq, Q_PAD), lambda b, h, i: (q_rows(b, h, i), h)),
            pl.BlockSpec((SEQ, LANES), lambda b, h, i: (b, 2 * h)),
            pl.BlockSpec((CTX_LEN, LANES), lambda b, h, i: (ctx_blk + b, 2 * h)),
            pl.BlockSpec((SEQ, LANES), lambda b, h, i: (b, PE_COLBLK)),
            pl.BlockSpec((CTX_LEN, LANES), lambda b, h, i: (ctx_blk + b, PE_COLBLK)),
            pl.BlockSpec((SEQ, LANES), lambda b, h, i: (b, 2 * h + 1)),
            pl.BlockSpec((CTX_LEN, LANES), lambda b, h, i: (ctx_blk + b, 2 * h + 1)),
            pl.BlockSpec((tq, LANES), lambda b, h, i: (jnp.minimum(i, nq_lat - 1), 0)),
            pl.BlockSpec((tq, LANES), lambda b, h, i: (jnp.minimum(i, nq_lat - 1), 0)),
            pl.BlockSpec((SEQ, LANES), lambda b, h, i: (0, 0)),
            pl.BlockSpec((SEQ, LANES), lambda b, h, i: (0, 0)),
            pl.BlockSpec((1, Q_PAD), lambda b, h, i: (0, 0)),
            pl.BlockSpec((1, LANES), lambda b, h, i: (0, 0)),
            pl.BlockSpec((1, LANES), lambda b, h, i: (0, 0)),
        ],
        out_specs=pl.BlockSpec((tq, LANES), lambda b, h, i: (q_rows(b, h, i), h)),
        scratch_shapes=[pltpu.VMEM((Q_PAD, SEQ + CTX_LEN), BF), pltpu.VMEM((SEQ + CTX_LEN, LANES), BF)],
        compiler_params=_cparams("parallel", "parallel", "arbitrary"),
        name="latent_attention",
    )(q3, kv3, kv3, p, p, kv3, kv3, cos_c, sin_c, cos_c, sin_c, gq, gkn, gkp)


def _fourier_in_kernel(u_ref, ab_ref, za_ref, zb_ref):
    for g in range(F_GROUPS):
        cols = slice(g * F_GROUP, (g + 1) * F_GROUP)
        z = jnp.dot(u_ref[:, cols].astype(BF), ab_ref[g], preferred_element_type=F32)
        za_ref[:, cols] = z[:, :F_GROUP].astype(za_ref.dtype)
        zb_ref[:, cols] = z[:, F_GROUP:].astype(zb_ref.dtype)


def _fourier_in(p, ab):
    tm = 512
    ucol = (Q_LORA + KV_LORA) // D_WIDTH
    return pl.pallas_call(
        _fourier_in_kernel,
        out_shape=[jax.ShapeDtypeStruct((T_ALL, D_WIDTH), BF)] * 2,
        grid=(T_ALL // tm,),
        in_specs=[pl.BlockSpec((tm, D_WIDTH), lambda i: (i, ucol)),
                  pl.BlockSpec((F_GROUPS, F_GROUP, 2 * F_GROUP), lambda i: (0, 0, 0))],
        out_specs=[pl.BlockSpec((tm, D_WIDTH), lambda i: (i, 0))] * 2,
        compiler_params=_cparams("parallel"),
        name="fourier_channel_mix",
    )(p, ab)


def _dft_kernel(cn_ref, sn_ref, za_ref, zb_ref, o_ref):
    o = (jnp.dot(cn_ref[...], za_ref[...], preferred_element_type=F32)
         - jnp.dot(sn_ref[...], zb_ref[...], preferred_element_type=F32))
    o_ref[...] = o.astype(o_ref.dtype)


def _dft(cn, sn, za, zb, seq, row_blk0, n_seq):
    tq = min(seq, 512)
    nt = seq // tq
    return pl.pallas_call(
        _dft_kernel,
        out_shape=jax.ShapeDtypeStruct((n_seq * seq, D_WIDTH), BF),
        grid=(nt, n_seq),
        in_specs=[
            pl.BlockSpec((tq, seq), lambda i, b: (i, 0)),
            pl.BlockSpec((tq, seq), lambda i, b: (i, 0)),
            pl.BlockSpec((seq, D_WIDTH), lambda i, b: (row_blk0 + b, 0)),
            pl.BlockSpec((seq, D_WIDTH), lambda i, b: (row_blk0 + b, 0)),
        ],
        out_specs=pl.BlockSpec((tq, D_WIDTH), lambda i, b: (b * nt + i, 0)),
        compiler_params=_cparams("parallel", "arbitrary"),
        name="fourier_position_dft",
    )(cn, sn, za, zb)


def _dft_matrices(n, scale):
    k = jnp.arange(n, dtype=jnp.int32)
    ang = ((k[:, None] * k[None, :]) % n).astype(F32) * (2.0 * math.pi / n)
    return jnp.cos(ang) * scale, jnp.sin(ang) * scale


def _out_ffn_kernel(y1_ref, y2_ref, w1_ref, w2_ref, x_ref, mod_ref, g_ref, wrh_ref, wrl_ref, br_ref,
                    xo_ref, f_ref, te_ref, tg_ref):
    acc = (jnp.dot(y1_ref[...], w1_ref[0], preferred_element_type=F32)
           + jnp.dot(y2_ref[...], w2_ref[0], preferred_element_type=F32))
    x1 = x_ref[...] + mod_ref[0, 2:3, :] * acc
    xo_ref[...] = x1
    y = x1 * lax.rsqrt(jnp.mean(x1 * x1, axis=-1, keepdims=True) + EPS) * g_ref[...]
    f = y * (1.0 + mod_ref[0, 4:5, :]) + mod_ref[0, 3:4, :]
    f_hi = f.astype(BF)
    f_hi32 = f_hi.astype(F32)
    _store_slabs(f_ref, f_hi32)
    f_lo = (f - f_hi32).astype(BF)
    wrh = wrh_ref[...]
    logits = (jnp.dot(f_hi, wrh, preferred_element_type=F32)
              + jnp.dot(f_lo, wrh, preferred_element_type=F32)
              + jnp.dot(f_hi, wrl_ref[...], preferred_element_type=F32)) + br_ref[...]
    tm = logits.shape[0]
    lane = lax.broadcasted_iota(jnp.int32, (tm, LANES), 1)
    lane_f = lane.astype(F32)
    neg = jnp.float32(-jnp.inf)
    l = jnp.where(lane < N_EXPERTS, logits, neg)
    vals, idxs = [], []
    for _ in range(TOP_K):
        m = jnp.max(l, axis=-1, keepdims=True)
        idx = jnp.min(jnp.where(l == m, lane_f, float(LANES)), axis=-1, keepdims=True)
        vals.append(m)
        idxs.append(idx)
        l = jnp.where(lane_f == idx, neg, l)
    es = [jnp.exp(v - vals[0]) for v in vals]
    tot = es[0] + es[1] + es[2] + es[3]
    te = jnp.zeros((tm, LANES), F32)
    tg = jnp.zeros((tm, LANES), F32)
    for k in range(TOP_K):
        te = jnp.where(lane == k, idxs[k], te)
        tg = jnp.where(lane == k, es[k] / tot, tg)
    te_ref[...] = te.astype(jnp.int32)
    tg_ref[...] = tg


def _out_ffn(y1, y2, w_out, wl, x, mod, g_ffn, w_router, b_router, t_out):
    tm = 256
    k1, k2 = y1.shape[1], y2.shape[1]
    d = D_MODEL
    wr = jnp.pad(w_router, ((0, 0), (0, LANES - N_EXPERTS)))
    wrh = wr.astype(BF)
    wrl = (wr - wrh.astype(F32)).astype(BF)
    br = jnp.pad(b_router, (0, LANES - N_EXPERTS)).reshape(1, LANES)
    const = lambda i: (0, 0)
    return pl.pallas_call(
        _out_ffn_kernel,
        out_shape=[jax.ShapeDtypeStruct((t_out, d), F32), jax.ShapeDtypeStruct((t_out * SLAB, LANES), F32),
                   jax.ShapeDtypeStruct((t_out, LANES), jnp.int32), jax.ShapeDtypeStruct((t_out, LANES), F32)],
        grid=(t_out // tm,),
        in_specs=[
            pl.BlockSpec((tm, k1), lambda i: (i, 0)),
            pl.BlockSpec((tm, k2), lambda i: (i, 0)),
            pl.BlockSpec((1, k1, d), lambda i: (wl, 0, 0)),
            pl.BlockSpec((1, k2, d), lambda i: (wl, k1 // k2, 0)),
            pl.BlockSpec((tm, d), lambda i: (i, 0)),
            pl.BlockSpec((1, 6, d), lambda i: (_row_group(i, tm), 0, 0)),
            pl.BlockSpec((1, d), const),
            pl.BlockSpec((d, LANES), const),
            pl.BlockSpec((d, LANES), const),
            pl.BlockSpec((1, LANES), const),
        ],
        out_specs=[pl.BlockSpec((tm, d), lambda i: (i, 0)), pl.BlockSpec((tm * SLAB, LANES), lambda i: (i, 0)),
                   pl.BlockSpec((tm, LANES), lambda i: (i, 0)), pl.BlockSpec((tm, LANES), lambda i: (i, 0))],
        compiler_params=_cparams("parallel"),
        name="out_proj_ffn_prep_router",
    )(y1, y2, w_out, w_out, x, mod, g_ffn.reshape(1, d), wrh, wrl, br)


CAST_ROWS = 256


def _cast_rows(dst, src):
    def body(c, carry):
        r0 = pl.multiple_of(c * CAST_ROWS, CAST_ROWS)
        dst[pl.ds(r0, CAST_ROWS), :] = src[0, 0, pl.ds(r0, CAST_ROWS), :].astype(BF)
        return carry

    lax.fori_loop(0, dst.shape[0] // CAST_ROWS, body, 0)


def _expert_kernel(be_ref, nu_ref, tokc_ref, tokn_ref, f_hbm, wg_ref, bg_ref, wd_ref, bd_ref, wt_ref, o_ref,
                   xbuf, sems, wg_bf, wd_bf):
    i = pl.program_id(0)
    n_used = nu_ref[0]
    slot = i % 2

    def row_copy(tok_ref, r, sl):
        src = f_hbm.at[pl.ds(pl.multiple_of(tok_ref[0, 0, r] * SLAB, SLAB), SLAB), :]
        return pltpu.make_async_copy(src, xbuf.at[sl, pl.ds(pl.multiple_of(r * SLAB, SLAB), SLAB), :], sems.at[sl])

    def gather(tok_ref, sl):
        def body(r, carry):
            row_copy(tok_ref, r, sl).start()
            return carry

        lax.fori_loop(0, MOE_TB, body, 0, unroll=8)

    @pl.when(jnp.logical_and(i == 0, n_used > 0))
    def _():
        gather(tokc_ref, 0)

    @pl.when(i + 1 < n_used)
    def _():
        gather(tokn_ref, 1 - slot)

    @pl.when(i < n_used)
    def _():
        @pl.when(jnp.logical_or(i == 0, be_ref[i] != be_ref[jnp.maximum(i - 1, 0)]))
        def _():
            _cast_rows(wg_bf, wg_ref)
            _cast_rows(wd_bf, wd_ref)

        pltpu.make_async_copy(f_hbm.at[pl.ds(0, MOE_TB * SLAB), :], xbuf.at[slot], sems.at[slot]).wait()
        x = _load_slabs(xbuf.at[slot], MOE_TB).astype(BF)
        gu = jnp.dot(x, wg_bf[...], preferred_element_type=F32) + bg_ref[0, 0]
        g_ = jnp.minimum(gu[:, :D_EXPERT], SWIGLU_LIMIT)
        up = jnp.clip(gu[:, D_EXPERT:], -SWIGLU_LIMIT, SWIGLU_LIMIT)
        act = (up + 1.0) * (g_ * jax.nn.sigmoid(SWIGLU_ALPHA * g_))
        y = jnp.dot(act.astype(BF), wd_bf[...], preferred_element_type=F32) + bd_ref[0, 0]
        _store_slabs(o_ref, y * wt_ref[:, 0:1])

    @pl.when(i >= n_used)
    def _():
        o_ref[...] = jnp.zeros(o_ref.shape, o_ref.dtype)


def _experts(f_slabs, tok, wgt, blk_e, n_used, w_gu, b_gu, w_down, b_down, wl):
    n_rows = tok.shape[0]
    d = D_MODEL
    n_blocks = n_rows // MOE_TB
    last = n_blocks - 1
    grid_spec = pltpu.PrefetchScalarGridSpec(
        num_scalar_prefetch=2,
        grid=(n_blocks,),
        in_specs=[
            pl.BlockSpec((1, 1, MOE_TB), lambda i, be, nu: (i, 0, 0), memory_space=pltpu.SMEM),
            pl.BlockSpec((1, 1, MOE_TB), lambda i, be, nu: (jnp.minimum(i + 1, last), 0, 0),
                         memory_space=pltpu.SMEM),
            pl.BlockSpec(memory_space=pl.ANY),
            pl.BlockSpec((1, 1, d, 2 * D_EXPERT), lambda i, be, nu: (wl, be[i], 0, 0)),
            pl.BlockSpec((1, 1, 1, 2 * D_EXPERT), lambda i, be, nu: (wl, be[i], 0, 0)),
            pl.BlockSpec((1, 1, D_EXPERT, d), lambda i, be, nu: (wl, be[i], 0, 0)),
            pl.BlockSpec((1, 1, 1, d), lambda i, be, nu: (wl, be[i], 0, 0)),
            pl.BlockSpec((MOE_TB, LANES), lambda i, be, nu: (i, 0)),
        ],
        out_specs=pl.BlockSpec((MOE_TB * SLAB, LANES), lambda i, be, nu: (i, 0)),
        scratch_shapes=[
            pltpu.VMEM((2, MOE_TB * SLAB, LANES), F32),
            pltpu.SemaphoreType.DMA((2,)),
            pltpu.VMEM((d, 2 * D_EXPERT), BF),
            pltpu.VMEM((D_EXPERT, d), BF),
        ],
    )
    tok3 = tok.reshape(n_blocks, 1, MOE_TB)
    return pl.pallas_call(
        _expert_kernel,
        out_shape=jax.ShapeDtypeStruct((n_rows * SLAB, LANES), F32),
        grid_spec=grid_spec,
        compiler_params=pltpu.CompilerParams(dimension_semantics=("arbitrary",), vmem_limit_bytes=EXPERT_VMEM_LIMIT),
        name="moe_experts",
    )(blk_e, n_used, tok3, tok3, f_slabs, w_gu, b_gu.reshape(DEPTH, N_EXPERTS, 1, 2 * D_EXPERT), w_down,
      b_down.reshape(DEPTH, N_EXPERTS, 1, d), jnp.broadcast_to(wgt[:, None], (n_rows, LANES)))


COMBINE_TT = 128


def _combine_kernel(slotc_ref, slotn_ref, y_hbm, x_ref, mod_ref, o_ref, buf, sems, sum_scr):
    i = pl.program_id(0)
    n = pl.num_programs(0)
    slot = i % 2
    rows = COMBINE_TT * SLAB

    def row_copy(slot_ref, a, sl):
        t, k = a // TOP_K, a % TOP_K
        src = y_hbm.at[pl.ds(pl.multiple_of(slot_ref[0, 0, a] * SLAB, SLAB), SLAB), :]
        dst = buf.at[sl, pl.ds(pl.multiple_of(k * rows + t * SLAB, SLAB), SLAB), :]
        return pltpu.make_async_copy(src, dst, sems.at[sl])

    def gather(slot_ref, sl):
        def body(a, carry):
            row_copy(slot_ref, a, sl).start()
            return carry

        lax.fori_loop(0, COMBINE_TT * TOP_K, body, 0, unroll=8)

    @pl.when(i == 0)
    def _():
        gather(slotc_ref, 0)

    @pl.when(i + 1 < n)
    def _():
        gather(slotn_ref, 1 - slot)

    pltpu.make_async_copy(y_hbm.at[pl.ds(0, TOP_K * rows), :], buf.at[slot], sems.at[slot]).wait()
    acc = buf[slot, 0:rows, :]
    for k in range(1, TOP_K):
        acc = acc + buf[slot, k * rows:(k + 1) * rows, :]
    sum_scr[...] = acc
    o_ref[...] = x_ref[...] + mod_ref[0, 5:6, :] * _load_slabs(sum_scr, COMBINE_TT)


def _combine(y_slabs, slot_of, x1, mod):
    t, d = x1.shape
    n = t // COMBINE_TT
    slot3 = slot_of.reshape(n, 1, COMBINE_TT * TOP_K)
    return pl.pallas_call(
        _combine_kernel,
        out_shape=jax.ShapeDtypeStruct((t, d), F32),
        grid=(n,),
        in_specs=[
            pl.BlockSpec((1, 1, COMBINE_TT * TOP_K), lambda i: (i, 0, 0), memory_space=pltpu.SMEM),
            pl.BlockSpec((1, 1, COMBINE_TT * TOP_K), lambda i: (jnp.minimum(i + 1, n - 1), 0, 0),
                         memory_space=pltpu.SMEM),
            pl.BlockSpec(memory_space=pl.ANY),
            pl.BlockSpec((COMBINE_TT, d), lambda i: (i, 0)),
            pl.BlockSpec((1, 6, d), lambda i: (_row_group(i, COMBINE_TT), 0, 0)),
        ],
        out_specs=pl.BlockSpec((COMBINE_TT, d), lambda i: (i, 0)),
        scratch_shapes=[
            pltpu.VMEM((2, TOP_K * COMBINE_TT * SLAB, LANES), F32),
            pltpu.SemaphoreType.DMA((2,)),
            pltpu.VMEM((COMBINE_TT * SLAB, LANES), F32),
        ],
        compiler_params=_cparams("arbitrary"),
        name="moe_combine",
    )(slot3, slot3, y_slabs, x1, mod)


def _route(top_e, gate):
    t = top_e.shape[0]
    n_assign = t * TOP_K
    n_blocks = (n_assign + N_EXPERTS * (MOE_TB - 1) + MOE_TB - 1) // MOE_TB
    n_rows = n_blocks * MOE_TB
    i32 = jnp.int32
    e_flat = top_e.reshape(-1)
    order = jnp.argsort(e_flat, stable=True).astype(i32)
    inv = jnp.argsort(order).astype(i32)
    counts = jnp.sum((e_flat[:, None] == jnp.arange(N_EXPERTS, dtype=i32)[None, :]).astype(i32), axis=0)
    padded = (counts + MOE_TB - 1) // MOE_TB * MOE_TB
    pad_end = jnp.cumsum(padded)
    start = pad_end - padded
    cstart = jnp.cumsum(counts) - counts
    blk_start = jnp.arange(n_blocks, dtype=i32) * MOE_TB
    blk_e = jnp.minimum(jnp.sum((pad_end[None, :] <= blk_start[:, None]).astype(i32), axis=1), N_EXPERTS - 1)
    n_used = (pad_end[-1] // MOE_TB).astype(i32).reshape(1)
    row = jnp.arange(n_rows, dtype=i32)
    e_row = jnp.repeat(blk_e, MOE_TB)
    idx_in = row - start[e_row]
    valid = jnp.logical_and(idx_in < counts[e_row], row < pad_end[-1])
    assign = order[jnp.clip(cstart[e_row] + idx_in, 0, n_assign - 1)]
    tok = jnp.where(valid, assign // TOP_K, 0).astype(i32)
    wgt = jnp.where(valid, gate.reshape(-1)[assign], 0.0)
    slot_of = (start[e_flat] + inv - cstart[e_flat]).astype(i32).reshape(t, TOP_K)
    return tok, wgt, slot_of, blk_e.astype(i32), n_used


def kernel(x, c, ctx, c_ctx, w_mod, b_mod, g_mix, g_ffn, w_out, w_in_ab, g_aq, g_ak, lam, g_subln, w_pool, s_pool, w_in_cd, g_qa, g_kva, w_qb, w_kvb, g_mq, g_mk, w_fourier, w_router, b_router, w_gu, b_gu, w_down, b_down):
    d = D_MODEL
    cc = jnp.concatenate([c, c_ctx[None, :], jnp.zeros((8 - BATCH - 1, d), F32)], axis=0)
    mod_all = _modulation(cc, w_mod, b_mod)
    cos_a, sin_a, cos_c, sin_c = _rope_tables()
    ctx_blk0_pool = T_LAT // CTX_LEN

    w_out_bf = w_out.astype(BF)
    w_in_ab_bf = w_in_ab.astype(BF)
    w_in_cd_bf = jnp.concatenate([w_in_cd[:, :, :C_PE_OFF], w_in_cd[:, :, D_OFF:], w_in_cd[:, :, C_PE_OFF:D_OFF],
                                  jnp.zeros((w_in_cd.shape[0], d, ODD_IN_PAD - ODD_IN), F32)], axis=2).astype(BF)
    w_qb_bf = jnp.pad(w_qb.reshape(-1, Q_LORA, C_HEADS, QK_DIM), ((0, 0), (0, 0), (0, 0), (0, Q_PAD - QK_DIM)))
    w_qb_bf = w_qb_bf.reshape(-1, Q_LORA, C_HEADS * Q_PAD).astype(BF)
    w_kvb_bf = w_kvb.astype(BF)

    xs = jnp.concatenate([x.reshape(T_LAT, d), ctx.reshape(T_CTX, d)], axis=0)
    for l in range(DEPTH):
        ctx_out = l < DEPTH - 1
        t_out = T_ALL if ctx_out else T_LAT
        i = l // 2
        mod = mod_all[l]
        if l % 2 == 0:
            lam_init = 0.8 - 0.6 * math.exp(-0.3 * l)
            p = _nm_matmul(xs, 0, d, g_mix[l], mod, w_in_ab_bf, i, 512, 1024, F32, "in_proj_even")
            y1 = _attn_a(p, cos_a, sin_a, g_aq[i], g_ak[i], lam[i], g_subln[i], lam_init, ctx_out)
            wp = w_pool[i].astype(BF)
            y2 = _pool(p, wp, s_pool[i], SEQ, 0, BATCH)
            if ctx_out:
                y2 = jnp.concatenate([y2, _pool(p, wp, s_pool[i], CTX_LEN, ctx_blk0_pool, BATCH)], axis=0)
        else:
            p = _nm_matmul(xs, 0, d, g_mix[l], mod, w_in_cd_bf, i, 512, ODD_IN_PAD, F32, "in_proj_odd")
            q3 = _nm_matmul(p, 0, Q_LORA, g_qa[i], None, w_qb_bf, i, 512, C_HEADS * Q_PAD, F32, "mla_q_up")
            kv3 = _nm_matmul(p, 1, KV_LORA, g_kva[i], None, w_kvb_bf, i, 512,
                             C_HEADS * (NOPE_DIM + V_DIM), F32, "mla_kv_up")
            y1 = _attn_c(q3, kv3, p, cos_c, sin_c, g_mq[i], g_mk[i], ctx_out)
            cc_m, sc_m = _dft_matrices(F_GROUP, 1.0)
            hp = lax.Precision.HIGHEST
            ab = jnp.concatenate([jnp.einsum('ck,gke->gce', cc_m, w_fourier[i], precision=hp),
                                  jnp.einsum('ck,gke->gce', sc_m, w_fourier[i], precision=hp)], axis=-1).astype(BF)
            za, zb = _fourier_in(p, ab)
            cn, sn = _dft_matrices(SEQ, (SEQ * F_GROUP) ** -0.5)
            y2 = _dft(cn.astype(BF), sn.astype(BF), za, zb, SEQ, 0, BATCH)
            if ctx_out:
                cn_c, sn_c = _dft_matrices(CTX_LEN, (CTX_LEN * F_GROUP) ** -0.5)
                y2 = jnp.concatenate([y2, _dft(cn_c.astype(BF), sn_c.astype(BF), za, zb, CTX_LEN,
                                               T_LAT // CTX_LEN, BATCH)], axis=0)

        x1, f_slabs, te, tg = _out_ffn(y1, y2, w_out_bf, l, xs, mod, g_ffn[l], w_router[l], b_router[l], t_out)
        tok, wgt, slot_of, blk_e, n_used = _route(te[:, :TOP_K], tg[:, :TOP_K])
        y_slabs = _experts(f_slabs, tok, wgt, blk_e, n_used, w_gu, b_gu, w_down, b_down, l)
        xs = _combine(y_slabs, slot_of, x1, mod)
    return xs.reshape(BATCH, SEQ, d)
```
